```python
import math
import jax, jax.numpy as jnp
from jax import lax
import numpy as np

D_MODEL = 1024
BATCH = 2
SEQ = 8192
DEPTH = 1

CONV_MIX_WIDTH = 1024
SHORT_CONV_WIDTH = 3
N_HEADS = 8
HEAD_DIM = 128
ATTN_WIDTH = N_HEADS * HEAD_DIM
MOBA_BLOCK = 256
MOBA_TOP_K = 3
Q_CHUNK = 32
ROPE_THETA = 10000.0
D_FF = 2816
FFN_CONV_WIDTH = 3

RMS_EPS = 1e-6
NEG_INF = -1e30
PROJ_WIDTH = 3 * CONV_MIX_WIDTH + 3 * ATTN_WIDTH + 2 * D_MODEL

kernel_name = "hybrid_gated_conv_moba_convffn"


def rmsnorm(x, g):
    xf = x.astype(jnp.float32)
    y = xf * lax.rsqrt(jnp.mean(xf * xf, axis=-1, keepdims=True) + RMS_EPS)
    return (y * g.astype(jnp.float32)).astype(x.dtype)


def causal_dwconv(x, w):
    width = w.shape[0]
    s = x.shape[1]
    xp = jnp.pad(x, ((0, 0), (width - 1, 0), (0, 0)))
    return sum(xp[:, i:i + s, :] * w[i] for i in range(width))


def rope(x, positions):
    hd = x.shape[-1]
    half = hd // 2
    inv_freq = jnp.exp(-math.log(ROPE_THETA) * jnp.arange(half, dtype=jnp.float32) * (2.0 / hd))
    ang = positions.astype(jnp.float32)[:, None, :, None] * inv_freq
    cos, sin = jnp.cos(ang), jnp.sin(ang)
    xf = x.astype(jnp.float32)
    x1, x2 = xf[..., :half], xf[..., half:]
    out = jnp.concatenate([x1 * cos - x2 * sin, x2 * cos + x1 * sin], axis=-1)
    return out.astype(x.dtype)


def moba_attention(q, k, v):
    b, h, s, hd = q.shape
    n_blk = -(-s // MOBA_BLOCK)
    s_pad = n_blk * MOBA_BLOCK
    pad = ((0, 0), (0, 0), (0, s_pad - s), (0, 0))
    q, k, v = jnp.pad(q, pad), jnp.pad(k, pad), jnp.pad(v, pad)
    kb = k.reshape(b, h, n_blk, MOBA_BLOCK, hd)
    vb = v.reshape(b, h, n_blk, MOBA_BLOCK, hd)

    k_mean = jnp.mean(kb.astype(jnp.float32), axis=3)
    gate = jnp.einsum('bhsd,bhnd->bhsn', q.astype(jnp.float32), k_mean)
    q_blk_all = jnp.arange(s_pad) // MOBA_BLOCK
    past = jnp.arange(n_blk)[None, :] < q_blk_all[:, None]
    gate = jnp.where(past, gate, NEG_INF)
    k_sel = min(MOBA_TOP_K, n_blk)
    _, sel = lax.top_k(gate, k_sel)

    scale = hd ** -0.5
    b_ix = jnp.arange(b)[:, None, None, None]
    h_ix = jnp.arange(h)[None, :, None, None]

    def chunk(c):
        start = c * Q_CHUNK
        q_c = lax.dynamic_slice_in_dim(q, start, Q_CHUNK, axis=2)
        sel_c = lax.dynamic_slice_in_dim(sel, start, Q_CHUNK, axis=2)
        q_pos = start + jnp.arange(Q_CHUNK)
        blk_start = (start // MOBA_BLOCK) * MOBA_BLOCK
        k_g = kb[b_ix, h_ix, sel_c]
        v_g = vb[b_ix, h_ix, sel_c]
        s_sel = jnp.einsum('bhqd,bhqjkd->bhqjk', q_c, k_g,
                           preferred_element_type=jnp.float32) * scale
        valid = sel_c < (q_pos // MOBA_BLOCK)[None, None, :, None]
        s_sel = jnp.where(valid[..., None], s_sel, NEG_INF).reshape(b, h, Q_CHUNK, k_sel * MOBA_BLOCK)
        k_own = lax.dynamic_slice_in_dim(k, blk_start, MOBA_BLOCK, axis=2)
        v_own = lax.dynamic_slice_in_dim(v, blk_start, MOBA_BLOCK, axis=2)
        s_own = jnp.einsum('bhqd,bhkd->bhqk', q_c, k_own,
                           preferred_element_type=jnp.float32) * scale
        causal = (blk_start + jnp.arange(MOBA_BLOCK))[None, :] <= q_pos[:, None]
        s_own = jnp.where(causal, s_own, NEG_INF)
        p = jax.nn.softmax(jnp.concatenate([s_sel, s_own], axis=-1), axis=-1).astype(v.dtype)
        p_sel = p[..., :k_sel * MOBA_BLOCK].reshape(b, h, Q_CHUNK, k_sel, MOBA_BLOCK)
        p_own = p[..., k_sel * MOBA_BLOCK:]
        return (jnp.einsum('bhqjk,bhqjkd->bhqd', p_sel, v_g)
                + jnp.einsum('bhqk,bhkd->bhqd', p_own, v_own))

    out = lax.map(chunk, jnp.arange(s_pad // Q_CHUNK))
    out = out.transpose(1, 0, 3, 2, 4).reshape(b, s_pad, h, hd)[:, :s]
    return out.reshape(b, s, h * hd)


def hybrid_mixer(xn, positions, w_in, conv_w, w_conv_out, w_attn_out, w_out):
    b, s, _ = xn.shape
    proj = xn @ w_in
    cuts = np.cumsum([CONV_MIX_WIDTH] * 3 + [ATTN_WIDTH] * 3 + [D_MODEL])
    c_b, c_c, c_x, q, k, v, g_conv, g_attn = jnp.split(proj, [int(i) for i in cuts], axis=-1)
    y_conv = (c_b * causal_dwconv(c_c * c_x, conv_w)) @ w_conv_out
    to_heads = lambda t: t.reshape(b, s, N_HEADS, HEAD_DIM).transpose(0, 2, 1, 3)
    q = rope(to_heads(q), positions)
    k = rope(to_heads(k), positions)
    y_attn = moba_attention(q, k, to_heads(v)) @ w_attn_out
    merged = jax.nn.sigmoid(g_conv) * y_conv + jax.nn.sigmoid(g_attn) * y_attn
    return merged @ w_out


def conv_gated_mlp(xn, w_ffn_in, ffn_conv_w, ffn_conv_b, w_ffn_down):
    u, g = jnp.split(xn @ w_ffn_in, 2, axis=-1)
    g = causal_dwconv(g, ffn_conv_w) + ffn_conv_b
    return (jax.nn.silu(g) * u) @ w_ffn_down


def setup_inputs(seed: int = 0) -> dict:
    key = jax.random.key(seed)
    ks = jax.random.split(key, 16)
    f32 = jnp.float32

    def dense(k, fan_in, shape):
        return jax.random.normal(k, shape, f32) * fan_in ** -0.5

    def gain(k, shape):
        return 1.0 + 0.05 * jax.random.normal(k, shape, f32)

    x = jax.random.normal(ks[0], (BATCH, SEQ, D_MODEL), f32)
    positions = jnp.broadcast_to(jnp.arange(SEQ, dtype=jnp.int32)[None, :], (BATCH, SEQ))
    return {
        "x": x,
        "positions": positions,
        "norm_mix_g": gain(ks[1], (DEPTH, D_MODEL)),
        "w_in": dense(ks[2], D_MODEL, (DEPTH, D_MODEL, PROJ_WIDTH)),
        "conv_w": dense(ks[3], SHORT_CONV_WIDTH, (DEPTH, SHORT_CONV_WIDTH, CONV_MIX_WIDTH)),
        "w_conv_out": dense(ks[4], CONV_MIX_WIDTH, (DEPTH, CONV_MIX_WIDTH, D_MODEL)),
        "w_attn_out": dense(ks[5], ATTN_WIDTH, (DEPTH, ATTN_WIDTH, D_MODEL)),
        "w_out": dense(ks[6], D_MODEL, (DEPTH, D_MODEL, D_MODEL)),
        "norm_ffn_g": gain(ks[7], (DEPTH, D_MODEL)),
        "w_ffn_in": dense(ks[8], D_MODEL, (DEPTH, D_MODEL, 2 * D_FF)),
        "ffn_conv_w": dense(ks[9], FFN_CONV_WIDTH, (DEPTH, FFN_CONV_WIDTH, D_FF)),
        "ffn_conv_b": 0.02 * jax.random.normal(ks[10], (DEPTH, D_FF), f32),
        "w_ffn_down": dense(ks[11], D_FF, (DEPTH, D_FF, D_MODEL)),
        "norm_final_g": gain(ks[12], (D_MODEL,)),
    }


def reference(x, positions, norm_mix_g, w_in, conv_w, w_conv_out, w_attn_out, w_out,
              norm_ffn_g, w_ffn_in, ffn_conv_w, ffn_conv_b, w_ffn_down, norm_final_g):
    h = x
    for layer in range(DEPTH):
        xn = rmsnorm(h, norm_mix_g[layer])
        h = h + hybrid_mixer(xn, positions, w_in[layer], conv_w[layer], w_conv_out[layer],
                             w_attn_out[layer], w_out[layer])
        xn = rmsnorm(h, norm_ffn_g[layer])
        h = h + conv_gated_mlp(xn, w_ffn_in[layer], ffn_conv_w[layer], ffn_conv_b[layer],
                               w_ffn_down[layer])
    return rmsnorm(h, norm_final_g)
```

```python
import functools
import math

import jax
import jax.numpy as jnp
from jax import lax
from jax.experimental import pallas as pl
from jax.experimental.pallas import tpu as pltpu

D_MODEL = 1024
N_HEADS = 8
HEAD_DIM = 128
MOBA_BLOCK = 256
MOBA_TOP_K = 3
ROPE_THETA = 10000.0
D_FF = 2816
RMS_EPS = 1e-6
NEG_INF = -1e30
N_SEG = 8

F32 = jnp.float32
BF16 = jnp.bfloat16

IN_PROJ_ROWS = 512
OUT_FFN_ROWS = 256
CONV_HALO_ROWS = 8
VMEM_LIMIT = 56 * 1024 * 1024


def _shifted_rows(prev_tail, cur):
    rows = cur.shape[0]
    stacked = jnp.concatenate([prev_tail, cur], axis=0)
    back1 = stacked[CONV_HALO_ROWS - 1:CONV_HALO_ROWS - 1 + rows]
    back2 = stacked[CONV_HALO_ROWS - 2:CONV_HALO_ROWS - 2 + rows]
    return back1, back2


def _in_proj_kernel(x_ref, pos_ref, gain_ref, freq_ref, sign_ref, w_ref, convw_ref,
                    a_ref, q_ref, k_ref, vt_ref, sgc_ref, sga_ref, tail_ref):
    si = pl.program_id(1)
    rows = x_ref.shape[0]

    x = x_ref[...]
    inv_rms = lax.rsqrt(jnp.mean(x * x, axis=-1, keepdims=True) + RMS_EPS)
    xn = (x * inv_rms * gain_ref[...]).astype(BF16)

    def seg(n):
        return jnp.dot(xn, w_ref[:, n * D_MODEL:(n + 1) * D_MODEL], preferred_element_type=F32)

    c_b = seg(0)
    u = seg(1) * seg(2)
    prev_tail = jnp.where(si == 0, 0.0, tail_ref[...])
    back1, back2 = _shifted_rows(prev_tail, u)
    conv = convw_ref[2:3, :] * u + convw_ref[1:2, :] * back1 + convw_ref[0:1, :] * back2
    a_ref[...] = (c_b * conv).astype(BF16)
    tail_ref[...] = u[rows - CONV_HALO_ROWS:, :]

    ang = pos_ref[...].astype(F32) * freq_ref[...]
    cos = jnp.cos(ang)
    sin_signed = jnp.sin(ang) * sign_ref[...]

    def rope_store(dst_ref, t, scale):
        for h in range(N_HEADS):
            th = t[:, h * HEAD_DIM:(h + 1) * HEAD_DIM]
            r = th * cos + pltpu.roll(th, HEAD_DIM // 2, 1) * sin_signed
            if scale is not None:
                r = r * scale
            dst_ref[:, h * HEAD_DIM:(h + 1) * HEAD_DIM] = r.astype(BF16)

    rope_store(q_ref, seg(3), HEAD_DIM ** -0.5)
    rope_store(k_ref, seg(4), None)
    vt_ref[...] = seg(5).T.astype(BF16)

    sgc_ref[...] = (1.0 / (1.0 + jnp.exp(-seg(6)))).astype(BF16)
    sga_ref[...] = (1.0 / (1.0 + jnp.exp(-seg(7)))).astype(BF16)


def _in_proj(x, positions, gain, w_in, conv_w):
    b, s, d = x.shape
    tm = IN_PROJ_ROWS
    half = HEAD_DIM // 2
    inv_freq = jnp.exp(-math.log(ROPE_THETA) * jnp.arange(half, dtype=F32) * (2.0 / HEAD_DIM))
    freq = jnp.concatenate([inv_freq, inv_freq])[None, :]
    sign = jnp.concatenate([-jnp.ones((half,), F32), jnp.ones((half,), F32)])[None, :]

    row_spec = pl.BlockSpec((None, tm, d), lambda bi, si: (bi, si, 0))
    const = lambda shape: pl.BlockSpec(shape, lambda bi, si: (0,) * len(shape),
                                       pipeline_mode=pl.Buffered(1))
    act = jax.ShapeDtypeStruct((b, s, d), BF16)
    return pl.pallas_call(
        _in_proj_kernel,
        grid=(b, s // tm),
        in_specs=[
            row_spec,
            pl.BlockSpec((None, tm, 1), lambda bi, si: (bi, si, 0)),
            const((1, d)),
            const((1, HEAD_DIM)),
            const((1, HEAD_DIM)),
            const((d, N_SEG * d)),
            const((3, d)),
        ],
        out_specs=[
            row_spec, row_spec, row_spec,
            pl.BlockSpec((None, d, tm), lambda bi, si: (bi, 0, si)),
            row_spec, row_spec,
        ],
        out_shape=[act, act, act, jax.ShapeDtypeStruct((b, d, s), BF16), act, act],
        scratch_shapes=[pltpu.VMEM((CONV_HALO_ROWS, d), F32)],
        compiler_params=pltpu.CompilerParams(
            dimension_semantics=("arbitrary", "arbitrary"),
            vmem_limit_bytes=VMEM_LIMIT),
        name="in_proj",
    )(x, positions[..., None], gain[None, :], freq, sign, w_in.astype(BF16), conv_w)


def _moba_kernel(q_ref, k_ref, vt_ref, o_ref, kmean_ref, bias_ref):
    qi = pl.program_id(2)
    n_blk = k_ref.shape[0] // MOBA_BLOCK
    nt = (((1,), (1,)), ((), ()))

    @pl.when(qi == 0)
    def _():
        k_all = k_ref[...].astype(F32).reshape(n_blk, MOBA_BLOCK, HEAD_DIM)
        km = jnp.sum(k_all, axis=1) * (1.0 / MOBA_BLOCK)
        hi = km.astype(BF16)
        kmean_ref[0:n_blk, :] = hi
        kmean_ref[n_blk:2 * n_blk, :] = (km - hi.astype(F32)).astype(BF16)

    q = q_ref[...]

    g2 = lax.dot_general(kmean_ref[...], q, nt, preferred_element_type=F32)
    gate = g2[0:n_blk, :] + g2[n_blk:2 * n_blk, :]
    blk = lax.broadcasted_iota(jnp.int32, gate.shape, 0)
    past = blk < qi
    gate = jnp.where(past, gate, NEG_INF)
    rank = jnp.zeros(gate.shape, jnp.int32)
    for other in range(n_blk):
        g_o = gate[other:other + 1, :]
        beats = (g_o > gate) | ((g_o == gate) & (other < blk))
        rank = rank + beats.astype(jnp.int32)
    selected = (rank < MOBA_TOP_K) & past
    bias_ref[...] = jnp.where(selected, 0.0, NEG_INF)

    start = pl.multiple_of(qi * MOBA_BLOCK, MOBA_BLOCK)
    k_own = k_ref[pl.ds(start, MOBA_BLOCK), :]
    s = lax.dot_general(k_own, q, nt, preferred_element_type=F32)
    key_i = lax.broadcasted_iota(jnp.int32, s.shape, 0)
    qry_i = lax.broadcasted_iota(jnp.int32, s.shape, 1)
    s = jnp.where(key_i <= qry_i, s, NEG_INF)
    m = jnp.max(s, axis=0, keepdims=True)
    p = jnp.exp(s - m)
    l = jnp.sum(p, axis=0, keepdims=True)
    acc = jnp.dot(vt_ref[:, pl.ds(start, MOBA_BLOCK)], p.astype(BF16), preferred_element_type=F32)

    def past_block(j, carry):
        m, l, acc = carry
        off = pl.multiple_of(j * MOBA_BLOCK, MOBA_BLOCK)
        s = lax.dot_general(k_ref[pl.ds(off, MOBA_BLOCK), :], q, nt, preferred_element_type=F32)
        s = s + bias_ref[pl.ds(j, 1), :]
        m_new = jnp.maximum(m, jnp.max(s, axis=0, keepdims=True))
        alpha = jnp.exp(m - m_new)
        p = jnp.exp(s - m_new)
        l = alpha * l + jnp.sum(p, axis=0, keepdims=True)
        pv = jnp.dot(vt_ref[:, pl.ds(off, MOBA_BLOCK)], p.astype(BF16), preferred_element_type=F32)
        return m_new, l, alpha * acc + pv

    m, l, acc = lax.fori_loop(0, qi, past_block, (m, l, acc))
    o_ref[...] = (acc * (1.0 / l)).T.astype(BF16)


def _moba_attention(q, k, vt):
    b, s, d = q.shape
    n_q = s // MOBA_BLOCK
    return pl.pallas_call(
        _moba_kernel,
        grid=(b, N_HEADS, n_q),
        in_specs=[
            pl.BlockSpec((None, MOBA_BLOCK, HEAD_DIM), lambda bi, h, qi: (bi, qi, h)),
            pl.BlockSpec((None, s, HEAD_DIM), lambda bi, h, qi: (bi, 0, h)),
            pl.BlockSpec((None, HEAD_DIM, s), lambda bi, h, qi: (bi, h, 0)),
        ],
        out_specs=pl.BlockSpec((None, MOBA_BLOCK, HEAD_DIM), lambda bi, h, qi: (bi, qi, h)),
        out_shape=jax.ShapeDtypeStruct((b, s, d), BF16),
        scratch_shapes=[
            pltpu.VMEM((2 * n_q, HEAD_DIM), BF16),
            pltpu.VMEM((n_q, MOBA_BLOCK), F32),
        ],
        compiler_params=pltpu.CompilerParams(
            dimension_semantics=("arbitrary", "arbitrary", "arbitrary"),
            vmem_limit_bytes=VMEM_LIMIT),
        name="moba_attn",
    )(q, k, vt)


def _rms(x, gain):
    return x * lax.rsqrt(jnp.mean(x * x, axis=-1, keepdims=True) + RMS_EPS) * gain


def _out_ffn_kernel(x_ref, a_ref, att_ref, sgc_ref, sga_ref, wc_ref, wa_ref, wo_ref,
                    g2_ref, wu_ref, wg_ref, fcw_ref, fcb_ref, wd_ref, g3_ref,
                    o_ref, tail_ref):
    si = pl.program_id(1)
    rows = x_ref.shape[0]

    y_conv = jnp.dot(a_ref[...], wc_ref[...], preferred_element_type=F32)
    y_attn = jnp.dot(att_ref[...], wa_ref[...], preferred_element_type=F32)
    merged = sgc_ref[...].astype(F32) * y_conv + sga_ref[...].astype(F32) * y_attn
    h = x_ref[...] + jnp.dot(merged.astype(BF16), wo_ref[...], preferred_element_type=F32)

    xn = _rms(h, g2_ref[...]).astype(BF16)
    up = jnp.dot(xn, wu_ref[...], preferred_element_type=F32)
    g = jnp.dot(xn, wg_ref[...], preferred_element_type=F32)
    prev_tail = jnp.where(si == 0, 0.0, tail_ref[...])
    back1, back2 = _shifted_rows(prev_tail, g)
    gc = (fcw_ref[2:3, :] * g + fcw_ref[1:2, :] * back1 + fcw_ref[0:1, :] * back2) + fcb_ref[...]
    tail_ref[...] = g[rows - CONV_HALO_ROWS:, :]
    act = (gc * (1.0 / (1.0 + jnp.exp(-gc))) * up).astype(BF16)
    h = h + jnp.dot(act, wd_ref[...], preferred_element_type=F32)

    o_ref[...] = _rms(h, g3_ref[...])


def _out_ffn(x, a, att, sgc, sga, w_conv_out, w_attn_out, w_out, gain_ffn, w_ffn_in,
             ffn_conv_w, ffn_conv_b, w_ffn_down, gain_final):
    b, s, d = x.shape
    tm = OUT_FFN_ROWS
    row_spec = pl.BlockSpec((None, tm, d), lambda bi, si: (bi, si, 0))
    const = lambda shape: pl.BlockSpec(shape, lambda bi, si: (0,) * len(shape),
                                       pipeline_mode=pl.Buffered(1))
    return pl.pallas_call(
        _out_ffn_kernel,
        grid=(b, s // tm),
        in_specs=[
            row_spec, row_spec, row_spec, row_spec, row_spec,
            const((d, d)), const((d, d)), const((d, d)),
            const((1, d)),
            const((d, D_FF)), const((d, D_FF)),
            const((3, D_FF)), const((1, D_FF)),
            const((D_FF, d)),
            const((1, d)),
        ],
        out_specs=row_spec,
        out_shape=jax.ShapeDtypeStruct((b, s, d), F32),
        scratch_shapes=[pltpu.VMEM((CONV_HALO_ROWS, D_FF), F32)],
        compiler_params=pltpu.CompilerParams(
            dimension_semantics=("arbitrary", "arbitrary"),
            vmem_limit_bytes=VMEM_LIMIT),
        name="out_ffn",
    )(x, a, att, sgc, sga,
      w_conv_out.astype(BF16), w_attn_out.astype(BF16), w_out.astype(BF16),
      gain_ffn[None, :],
      w_ffn_in[:, :D_FF].astype(BF16), w_ffn_in[:, D_FF:].astype(BF16),
      ffn_conv_w, ffn_conv_b[None, :], w_ffn_down.astype(BF16), gain_final[None, :])


@jax.jit
def kernel(x, positions, norm_mix_g, w_in, conv_w, w_conv_out, w_attn_out, w_out, norm_ffn_g,
           w_ffn_in, ffn_conv_w, ffn_conv_b, w_ffn_down, norm_final_g):
    assert norm_mix_g.shape[0] == 1, "single-layer stack"
    a, q, k, vt, sgc, sga = _in_proj(x, positions, norm_mix_g[0], w_in[0], conv_w[0])
    att = _moba_attention(q, k, vt)
    return _out_ffn(x, a, att, sgc, sga, w_conv_out[0], w_attn_out[0], w_out[0], norm_ffn_g[0],
                    w_ffn_in[0], ffn_conv_w[0], ffn_conv_b[0], w_ffn_down[0], norm_final_g)
```

```python
import functools
import math

import jax
import jax.numpy as jnp
from jax import lax
from jax.experimental import pallas as pl
from jax.experimental.pallas import tpu as pltpu

D_MODEL = 1024
N_HEADS = 8
HEAD_DIM = 128
MOBA_BLOCK = 256
MOBA_TOP_K = 3
ROPE_THETA = 10000.0
D_FF = 2816
RMS_EPS = 1e-6
NEG_INF = -1e30
SCORE_SCALE = HEAD_DIM ** -0.5 * math.log2(math.e)
N_SEG = 8

F32 = jnp.float32
BF16 = jnp.bfloat16

IN_PROJ_ROWS = 512
HEAD_GROUP = 4
OUT_FFN_ROWS = 256
CONV_HALO_ROWS = 8
VMEM_LIMIT = 56 * 1024 * 1024


def _shifted_rows(prev_tail, cur):
    rows = cur.shape[0]
    stacked = jnp.concatenate([prev_tail, cur], axis=0)
    back1 = stacked[CONV_HALO_ROWS - 1:CONV_HALO_ROWS - 1 + rows]
    back2 = stacked[CONV_HALO_ROWS - 2:CONV_HALO_ROWS - 2 + rows]
    return back1, back2


def _in_proj_kernel(x_ref, pos_ref, gain_ref, freq_ref, sign_ref, w_ref, convw_ref,
                    a_ref, q_ref, k_ref, vt_ref, sgc_ref, sga_ref, tail_ref):
    si = pl.program_id(1)
    rows = x_ref.shape[0]

    x = x_ref[...]
    inv_rms = lax.rsqrt(jnp.mean(x * x, axis=-1, keepdims=True) + RMS_EPS)
    xn = (x * inv_rms * gain_ref[...]).astype(BF16)

    def seg(n):
        return jnp.dot(xn, w_ref[:, n * D_MODEL:(n + 1) * D_MODEL], preferred_element_type=F32)

    c_b = seg(0)
    u = seg(1) * seg(2)
    prev_tail = jnp.where(si == 0, 0.0, tail_ref[...])
    back1, back2 = _shifted_rows(prev_tail, u)
    conv = convw_ref[2:3, :] * u + convw_ref[1:2, :] * back1 + convw_ref[0:1, :] * back2
    a_ref[...] = (c_b * conv).astype(BF16)
    tail_ref[...] = u[rows - CONV_HALO_ROWS:, :]

    ang = pos_ref[...].astype(F32) * freq_ref[...]
    cos = jnp.cos(ang)
    sin_signed = jnp.sin(ang) * sign_ref[...]

    def rope_store(dst_ref, t, scale):
        for h in range(N_HEADS):
            th = t[:, h * HEAD_DIM:(h + 1) * HEAD_DIM]
            r = th * cos + pltpu.roll(th, HEAD_DIM // 2, 1) * sin_signed
            if scale is not None:
                r = r * scale
            dst_ref[:, h * HEAD_DIM:(h + 1) * HEAD_DIM] = r.astype(BF16)

    rope_store(q_ref, seg(3), SCORE_SCALE)
    rope_store(k_ref, seg(4), None)
    vt_ref[...] = seg(5).T.astype(BF16)

    sgc_ref[...] = (1.0 / (1.0 + jnp.exp(-seg(6)))).astype(BF16)
    sga_ref[...] = (1.0 / (1.0 + jnp.exp(-seg(7)))).astype(BF16)


def _in_proj(x, positions, gain, w_in, conv_w):
    b, s, d = x.shape
    tm = IN_PROJ_ROWS
    half = HEAD_DIM // 2
    inv_freq = jnp.exp(-math.log(ROPE_THETA) * jnp.arange(half, dtype=F32) * (2.0 / HEAD_DIM))
    freq = jnp.concatenate([inv_freq, inv_freq])[None, :]
    sign = jnp.concatenate([-jnp.ones((half,), F32), jnp.ones((half,), F32)])[None, :]

    row_spec = pl.BlockSpec((None, tm, d), lambda bi, si: (bi, si, 0))
    const = lambda shape: pl.BlockSpec(shape, lambda bi, si: (0,) * len(shape),
                                       pipeline_mode=pl.Buffered(1))
    act = jax.ShapeDtypeStruct((b, s, d), BF16)
    return pl.pallas_call(
        _in_proj_kernel,
        grid=(b, s // tm),
        in_specs=[
            row_spec,
            pl.BlockSpec((None, tm, 1), lambda bi, si: (bi, si, 0)),
            const((1, d)),
            const((1, HEAD_DIM)),
            const((1, HEAD_DIM)),
            const((d, N_SEG * d)),
            const((3, d)),
        ],
        out_specs=[
            row_spec, row_spec, row_spec,
            pl.BlockSpec((None, d, tm), lambda bi, si: (bi, 0, si)),
            row_spec, row_spec,
        ],
        out_shape=[act, act, act, jax.ShapeDtypeStruct((b, d, s), BF16), act, act],
        scratch_shapes=[pltpu.VMEM((CONV_HALO_ROWS, d), F32)],
        compiler_params=pltpu.CompilerParams(
            dimension_semantics=("arbitrary", "arbitrary"),
            vmem_limit_bytes=VMEM_LIMIT),
        name="in_proj",
    )(x, positions[..., None], gain[None, :], freq, sign, w_in.astype(BF16), conv_w)


def _moba_kernel(q_ref, k_ref, vt_ref, o_ref, kmean_ref, bias_ref, m_ref, l_ref, acc_ref,
                 s_ref, tmax_ref, p_ref, alpha_ref):
    qi = pl.program_id(2)
    n_blk = k_ref.shape[0] // MOBA_BLOCK
    nt = (((1,), (1,)), ((), ()))
    heads = range(HEAD_GROUP)
    lanes = lambda h: slice(h * HEAD_DIM, (h + 1) * HEAD_DIM)

    @pl.when(qi == 0)
    def _():
        for h in heads:
            k_all = k_ref[:, lanes(h)].astype(F32).reshape(n_blk, MOBA_BLOCK, HEAD_DIM)
            km = jnp.sum(k_all, axis=1) * (1.0 / MOBA_BLOCK)
            hi = km.astype(BF16)
            kmean_ref[h, 0:n_blk, :] = hi
            kmean_ref[h, n_blk:2 * n_blk, :] = (km - hi.astype(F32)).astype(BF16)

    for h in heads:
        g2 = lax.dot_general(kmean_ref[h], q_ref[:, lanes(h)], nt, preferred_element_type=F32)
        gate = g2[0:n_blk, :] + g2[n_blk:2 * n_blk, :]
        blk = lax.broadcasted_iota(jnp.int32, gate.shape, 0)
        past = blk < qi
        gate = jnp.where(past, gate, NEG_INF)
        rank = jnp.zeros(gate.shape, jnp.int32)
        for other in range(n_blk):
            g_o = gate[other:other + 1, :]
            beats = (g_o > gate) | ((g_o == gate) & (other < blk))
            rank = rank + beats.astype(jnp.int32)
        selected = (rank < MOBA_TOP_K) & past
        bias_ref[h] = jnp.where(selected, 0.0, NEG_INF)
        m_ref[h] = jnp.full(m_ref.shape[1:], NEG_INF, F32)
        l_ref[h] = jnp.zeros(l_ref.shape[1:], F32)
        acc_ref[h] = jnp.zeros(acc_ref.shape[1:], F32)

    def key_block(t):
        return jnp.minimum(jnp.where(t == 0, qi, t - 1), n_blk - 1)

    def score_stage(blk, slot, own):
        off = pl.multiple_of(blk * MOBA_BLOCK, MOBA_BLOCK)
        for h in heads:
            s = lax.dot_general(k_ref[pl.ds(off, MOBA_BLOCK), lanes(h)], q_ref[:, lanes(h)], nt,
                                preferred_element_type=F32)
            if own:
                key_i = lax.broadcasted_iota(jnp.int32, s.shape, 0)
                qry_i = lax.broadcasted_iota(jnp.int32, s.shape, 1)
                s = jnp.where(key_i <= qry_i, s, NEG_INF)
            else:
                s = s + bias_ref[h, pl.ds(blk, 1), :]
            s_ref[slot, h] = s
            tmax_ref[slot, h] = jnp.max(s, axis=0, keepdims=True)

    def softmax_stage(slot):
        for h in heads:
            m = m_ref[h]
            m_new = jnp.maximum(m, tmax_ref[slot, h])
            alpha = jnp.exp2(m - m_new)
            p = jnp.exp2(s_ref[slot, h] - m_new)
            m_ref[h] = m_new
            l_ref[h] = alpha * l_ref[h] + jnp.sum(p, axis=0, keepdims=True)
            p_ref[slot, h] = p.astype(BF16)
            alpha_ref[slot, h] = alpha

    def value_stage(blk, slot):
        off = pl.multiple_of(blk * MOBA_BLOCK, MOBA_BLOCK)
        for h in heads:
            pv = jnp.dot(vt_ref[lanes(h), pl.ds(off, MOBA_BLOCK)], p_ref[slot, h],
                         preferred_element_type=F32)
            acc_ref[h] = alpha_ref[slot, h] * acc_ref[h] + pv

    score_stage(key_block(0), 0, True)
    score_stage(key_block(1), 1, False)
    softmax_stage(0)

    def tile_pair(u, carry):
        t = 2 * u
        score_stage(key_block(t + 2), 0, False)
        softmax_stage(1)
        value_stage(key_block(t), 0)
        score_stage(key_block(t + 3), 1, False)
        softmax_stage(0)
        value_stage(key_block(t + 1), 1)
        return carry

    lax.fori_loop(0, (qi + 2) // 2, tile_pair, 0)
    for h in heads:
        o_ref[:, lanes(h)] = (acc_ref[h] * (1.0 / l_ref[h])).T.astype(BF16)


def _moba_attention(q, k, vt):
    b, s, d = q.shape
    n_q = s // MOBA_BLOCK
    width = HEAD_GROUP * HEAD_DIM
    return pl.pallas_call(
        _moba_kernel,
        grid=(b, N_HEADS // HEAD_GROUP, n_q),
        in_specs=[
            pl.BlockSpec((None, MOBA_BLOCK, width), lambda bi, hg, qi: (bi, qi, hg)),
            pl.BlockSpec((None, s, width), lambda bi, hg, qi: (bi, 0, hg)),
            pl.BlockSpec((None, width, s), lambda bi, hg, qi: (bi, hg, 0)),
        ],
        out_specs=pl.BlockSpec((None, MOBA_BLOCK, width), lambda bi, hg, qi: (bi, qi, hg)),
        out_shape=jax.ShapeDtypeStruct((b, s, d), BF16),
        scratch_shapes=[
            pltpu.VMEM((HEAD_GROUP, 2 * n_q, HEAD_DIM), BF16),
            pltpu.VMEM((HEAD_GROUP, n_q, MOBA_BLOCK), F32),
            pltpu.VMEM((HEAD_GROUP, 1, MOBA_BLOCK), F32),
            pltpu.VMEM((HEAD_GROUP, 1, MOBA_BLOCK), F32),
            pltpu.VMEM((HEAD_GROUP, HEAD_DIM, MOBA_BLOCK), F32),
            pltpu.VMEM((2, HEAD_GROUP, MOBA_BLOCK, MOBA_BLOCK), F32),
            pltpu.VMEM((2, HEAD_GROUP, 1, MOBA_BLOCK), F32),
            pltpu.VMEM((2, HEAD_GROUP, MOBA_BLOCK, MOBA_BLOCK), BF16),
            pltpu.VMEM((2, HEAD_GROUP, 1, MOBA_BLOCK), F32),
        ],
        compiler_params=pltpu.CompilerParams(
            dimension_semantics=("arbitrary", "arbitrary", "arbitrary"),
            vmem_limit_bytes=VMEM_LIMIT),
        name="moba_attn",
    )(q, k, vt)


def _rms(x, gain):
    return x * lax.rsqrt(jnp.mean(x * x, axis=-1, keepdims=True) + RMS_EPS) * gain


def _out_ffn_kernel(x_ref, a_ref, att_ref, sgc_ref, sga_ref, wc_ref, wa_ref, wo_ref,
                    g2_ref, wu_ref, wg_ref, fcw_ref, fcb_ref, wd_ref, g3_ref,
                    o_ref, tail_ref):
    si = pl.program_id(1)
    rows = x_ref.shape[0]

    y_conv = jnp.dot(a_ref[...], wc_ref[...], preferred_element_type=F32)
    y_attn = jnp.dot(att_ref[...], wa_ref[...], preferred_element_type=F32)
    merged = sgc_ref[...].astype(F32) * y_conv + sga_ref[...].astype(F32) * y_attn
    h = x_ref[...] + jnp.dot(merged.astype(BF16), wo_ref[...], preferred_element_type=F32)

    xn = _rms(h, g2_ref[...]).astype(BF16)
    up = jnp.dot(xn, wu_ref[...], preferred_element_type=F32)
    g = jnp.dot(xn, wg_ref[...], preferred_element_type=F32)
    prev_tail = jnp.where(si == 0, 0.0, tail_ref[...])
    back1, back2 = _shifted_rows(prev_tail, g)
    gc = (fcw_ref[2:3, :] * g + fcw_ref[1:2, :] * back1 + fcw_ref[0:1, :] * back2) + fcb_ref[...]
    tail_ref[...] = g[rows - CONV_HALO_ROWS:, :]
    act = (gc * (1.0 / (1.0 + jnp.exp(-gc))) * up).astype(BF16)
    h = h + jnp.dot(act, wd_ref[...], preferred_element_type=F32)

    o_ref[...] = _rms(h, g3_ref[...])


def _out_ffn(x, a, att, sgc, sga, w_conv_out, w_attn_out, w_out, gain_ffn, w_ffn_in,
             ffn_conv_w, ffn_conv_b, w_ffn_down, gain_final):
    b, s, d = x.shape
    tm = OUT_FFN_ROWS
    row_spec = pl.BlockSpec((None, tm, d), lambda bi, si: (bi, si, 0))
    const = lambda shape: pl.BlockSpec(shape, lambda bi, si: (0,) * len(shape),
                                       pipeline_mode=pl.Buffered(1))
    return pl.pallas_call(
        _out_ffn_kernel,
        grid=(b, s // tm),
        in_specs=[
            row_spec, row_spec, row_spec, row_spec, row_spec,
            const((d, d)), const((d, d)), const((d, d)),
            const((1, d)),
            const((d, D_FF)), const((d, D_FF)),
            const((3, D_FF)), const((1, D_FF)),
            const((D_FF, d)),
            const((1, d)),
        ],
        out_specs=row_spec,
        out_shape=jax.ShapeDtypeStruct((b, s, d), F32),
        scratch_shapes=[pltpu.VMEM((CONV_HALO_ROWS, D_FF), F32)],
        compiler_params=pltpu.CompilerParams(
            dimension_semantics=("arbitrary", "arbitrary"),
            vmem_limit_bytes=VMEM_LIMIT),
        name="out_ffn",
    )(x, a, att, sgc, sga,
      w_conv_out.astype(BF16), w_attn_out.astype(BF16), w_out.astype(BF16),
      gain_ffn[None, :],
      w_ffn_in[:, :D_FF].astype(BF16), w_ffn_in[:, D_FF:].astype(BF16),
      ffn_conv_w, ffn_conv_b[None, :], w_ffn_down.astype(BF16), gain_final[None, :])


@jax.jit
def kernel(x, positions, norm_mix_g, w_in, conv_w, w_conv_out, w_attn_out, w_out, norm_ffn_g,
           w_ffn_in, ffn_conv_w, ffn_conv_b, w_ffn_down, norm_final_g):
    assert norm_mix_g.shape[0] == 1, "single-layer stack"
    a, q, k, vt, sgc, sga = _in_proj(x, positions, norm_mix_g[0], w_in[0], conv_w[0])
    att = _moba_attention(q, k, vt)
    return _out_ffn(x, a, att, sgc, sga, w_conv_out[0], w_attn_out[0], w_out[0], norm_ffn_g[0],
                    w_ffn_in[0], ffn_conv_w[0], ffn_conv_b[0], w_ffn_down[0], norm_final_g)
```

```python
import math

import jax
import jax.numpy as jnp
from jax import lax
from jax.experimental import pallas as pl
from jax.experimental.pallas import tpu as pltpu

D_MODEL = 1024
N_HEADS = 8
HEAD_DIM = 128
MOBA_BLOCK = 256
MOBA_TOP_K = 3
ROPE_THETA = 10000.0
D_FF = 2816
RMS_EPS = 1e-6
NEG_INF = -1e30
BELOW_ALL = -3e38
SCORE_SCALE = HEAD_DIM ** -0.5 * math.log2(math.e)
N_SEG = 8

F32 = jnp.float32
BF16 = jnp.bfloat16

IN_PROJ_ROWS = 512
HEAD_GROUP = 8
VT_ROWS = HEAD_DIM + 16
OUT_FFN_ROWS = 256
CONV_HALO_ROWS = 8
VMEM_LIMIT = 56 * 1024 * 1024


def _shifted_rows(prev_tail, cur):
    rows = cur.shape[0]
    stacked = jnp.concatenate([prev_tail, cur], axis=0)
    back1 = stacked[CONV_HALO_ROWS - 1:CONV_HALO_ROWS - 1 + rows]
    back2 = stacked[CONV_HALO_ROWS - 2:CONV_HALO_ROWS - 2 + rows]
    return back1, back2


def _in_proj_kernel(x_ref, pos_ref, gain_ref, freq_ref, sign_ref, w_ref, convw_ref,
                    a_ref, q_ref, k_ref, vt_ref, sgc_ref, sga_ref, tail_ref):
    si = pl.program_id(1)
    rows = x_ref.shape[0]

    x = x_ref[...]
    inv_rms = lax.rsqrt(jnp.mean(x * x, axis=-1, keepdims=True) + RMS_EPS)
    xn = (x * inv_rms * gain_ref[...]).astype(BF16)

    def seg(n):
        return jnp.dot(xn, w_ref[:, n * D_MODEL:(n + 1) * D_MODEL], preferred_element_type=F32)

    c_b = seg(0)
    u = seg(1) * seg(2)
    prev_tail = jnp.where(si == 0, 0.0, tail_ref[...])
    back1, back2 = _shifted_rows(prev_tail, u)
    conv = convw_ref[2:3, :] * u + convw_ref[1:2, :] * back1 + convw_ref[0:1, :] * back2
    a_ref[...] = (c_b * conv).astype(BF16)
    tail_ref[...] = u[rows - CONV_HALO_ROWS:, :]

    ang = pos_ref[...].astype(F32) * freq_ref[...]
    cos = jnp.cos(ang)
    sin_signed = jnp.sin(ang) * sign_ref[...]

    def rope_store(dst_ref, t, scale):
        for h in range(N_HEADS):
            th = t[:, h * HEAD_DIM:(h + 1) * HEAD_DIM]
            r = th * cos + pltpu.roll(th, HEAD_DIM // 2, 1) * sin_signed
            if scale is not None:
                r = r * scale
            dst_ref[:, h * HEAD_DIM:(h + 1) * HEAD_DIM] = r.astype(BF16)

    rope_store(q_ref, seg(3), SCORE_SCALE)
    rope_store(k_ref, seg(4), None)
    vt = seg(5).T.astype(BF16)
    for h in range(N_HEADS):
        vt_ref[h * VT_ROWS:h * VT_ROWS + HEAD_DIM, :] = vt[h * HEAD_DIM:(h + 1) * HEAD_DIM, :]
        vt_ref[h * VT_ROWS + HEAD_DIM:(h + 1) * VT_ROWS, :] = jnp.ones(
            (VT_ROWS - HEAD_DIM, rows), BF16)

    sgc_ref[...] = (1.0 / (1.0 + jnp.exp(-seg(6)))).astype(BF16)
    sga_ref[...] = (1.0 / (1.0 + jnp.exp(-seg(7)))).astype(BF16)


def _in_proj(x, positions, gain, w_in, conv_w):
    b, s, d = x.shape
    tm = IN_PROJ_ROWS
    half = HEAD_DIM // 2
    inv_freq = jnp.exp(-math.log(ROPE_THETA) * jnp.arange(half, dtype=F32) * (2.0 / HEAD_DIM))
    freq = jnp.concatenate([inv_freq, inv_freq])[None, :]
    sign = jnp.concatenate([-jnp.ones((half,), F32), jnp.ones((half,), F32)])[None, :]

    row_spec = pl.BlockSpec((None, tm, d), lambda bi, si: (bi, si, 0))
    const = lambda shape: pl.BlockSpec(shape, lambda bi, si: (0,) * len(shape),
                                       pipeline_mode=pl.Buffered(1))
    act = jax.ShapeDtypeStruct((b, s, d), BF16)
    return pl.pallas_call(
        _in_proj_kernel,
        grid=(b, s // tm),
        in_specs=[
            row_spec,
            pl.BlockSpec((None, tm, 1), lambda bi, si: (bi, si, 0)),
            const((1, d)),
            const((1, HEAD_DIM)),
            const((1, HEAD_DIM)),
            const((d, N_SEG * d)),
            const((3, d)),
        ],
        out_specs=[
            row_spec, row_spec, row_spec,
            pl.BlockSpec((None, N_HEADS * VT_ROWS, tm), lambda bi, si: (bi, 0, si)),
            row_spec, row_spec,
        ],
        out_shape=[act, act, act, jax.ShapeDtypeStruct((b, N_HEADS * VT_ROWS, s), BF16), act, act],
        scratch_shapes=[pltpu.VMEM((CONV_HALO_ROWS, d), F32)],
        compiler_params=pltpu.CompilerParams(
            dimension_semantics=("arbitrary", "arbitrary"),
            vmem_limit_bytes=VMEM_LIMIT),
        name="in_proj",
    )(x, positions[..., None], gain[None, :], freq, sign, w_in.astype(BF16), conv_w)


def _moba_kernel(q_ref, k_ref, vt_ref, o_ref, kmean_ref, qaug_ref, m_ref, acc_ref,
                 s_ref, tmax_ref, p_ref, alpha_ref):
    qi = pl.program_id(2)
    n_blk = k_ref.shape[0] // MOBA_BLOCK
    nt = (((1,), (1,)), ((), ()))
    heads = range(HEAD_GROUP)
    lanes = lambda h: slice(h * HEAD_DIM, (h + 1) * HEAD_DIM)
    vrows = lambda h: slice(h * VT_ROWS, (h + 1) * VT_ROWS)

    @pl.when(qi == 0)
    def _():
        for h in heads:
            k_all = k_ref[:, lanes(h)].astype(F32).reshape(n_blk, MOBA_BLOCK, HEAD_DIM)
            km = jnp.sum(k_all, axis=1) * (1.0 / MOBA_BLOCK)
            hi = km.astype(BF16)
            kmean_ref[h, 0:n_blk, :] = hi
            kmean_ref[h, n_blk:2 * n_blk, :] = (km - hi.astype(F32)).astype(BF16)

    for h in heads:
        q = q_ref[:, lanes(h)]
        g2 = lax.dot_general(kmean_ref[h], q, nt, preferred_element_type=F32)
        gate = g2[0:n_blk, :] + g2[n_blk:2 * n_blk, :]
        blk = lax.broadcasted_iota(jnp.int32, gate.shape, 0)
        past = blk < qi
        gate = jnp.where(past, gate, NEG_INF)
        selected = jnp.zeros(gate.shape, jnp.bool_)
        for _ in range(MOBA_TOP_K):
            best = jnp.max(gate, axis=0, keepdims=True)
            first = jnp.min(jnp.where(gate == best, blk, n_blk), axis=0, keepdims=True)
            pick = blk == first
            selected = selected | pick
            gate = jnp.where(pick, BELOW_ALL, gate)
        bias = jnp.where(selected & past, 0.0, NEG_INF).astype(BF16)
        qaug_ref[h, 0:HEAD_DIM, :] = q.T
        qaug_ref[h, HEAD_DIM:HEAD_DIM + n_blk, :] = bias
        qaug_ref[h, HEAD_DIM + n_blk:2 * HEAD_DIM, :] = jnp.zeros(
            (HEAD_DIM - n_blk, MOBA_BLOCK), BF16)
        m_ref[h] = jnp.full(m_ref.shape[1:], NEG_INF, F32)
        acc_ref[h] = jnp.zeros(acc_ref.shape[1:], F32)

    def key_block(t):
        return jnp.minimum(jnp.where(t == 0, qi, t - 1), n_blk - 1)

    def score_stage(blk, slot, own):
        off = pl.multiple_of(blk * MOBA_BLOCK, MOBA_BLOCK)
        lane = lax.broadcasted_iota(jnp.int32, (MOBA_BLOCK, HEAD_DIM), 1)
        onehot = jnp.where(lane == (-1 if own else blk), 1.0, 0.0).astype(BF16)
        for h in heads:
            k_aug = jnp.concatenate([k_ref[pl.ds(off, MOBA_BLOCK), lanes(h)], onehot], axis=1)
            s = jnp.dot(k_aug, qaug_ref[h], preferred_element_type=F32)
            if own:
                key_i = lax.broadcasted_iota(jnp.int32, s.shape, 0)
                qry_i = lax.broadcasted_iota(jnp.int32, s.shape, 1)
                s = jnp.where(key_i <= qry_i, s, NEG_INF)
            s_ref[slot, h] = s
            tmax_ref[slot, h] = jnp.max(s, axis=0, keepdims=True)

    def softmax_stage(slot):
        for h in heads:
            m = m_ref[h]
            m_new = jnp.maximum(m, tmax_ref[slot, h])
            m_ref[h] = m_new
            alpha_ref[slot, h] = jnp.exp2(m - m_new)
            p_ref[slot, h] = jnp.exp2(s_ref[slot, h] - m_new).astype(BF16)

    def value_stage(blk, slot):
        off = pl.multiple_of(blk * MOBA_BLOCK, MOBA_BLOCK)
        for h in heads:
            pv = jnp.dot(vt_ref[vrows(h), pl.ds(off, MOBA_BLOCK)], p_ref[slot, h],
                         preferred_element_type=F32)
            acc_ref[h] = alpha_ref[slot, h] * acc_ref[h] + pv

    score_stage(key_block(0), 0, True)
    score_stage(key_block(1), 1, False)
    softmax_stage(0)

    def tile_pair(u, carry):
        t = 2 * u
        score_stage(key_block(t + 2), 0, False)
        softmax_stage(1)
        value_stage(key_block(t), 0)
        score_stage(key_block(t + 3), 1, False)
        softmax_stage(0)
        value_stage(key_block(t + 1), 1)
        return carry

    lax.fori_loop(0, (qi + 2) // 2, tile_pair, 0)
    for h in heads:
        inv_l = 1.0 / acc_ref[h, HEAD_DIM:HEAD_DIM + 1, :]
        o_ref[:, lanes(h)] = (acc_ref[h, 0:HEAD_DIM, :] * inv_l).T.astype(BF16)


def _moba_attention(q, k, vt):
    b, s, d = q.shape
    n_q = s // MOBA_BLOCK
    width = HEAD_GROUP * HEAD_DIM
    return pl.pallas_call(
        _moba_kernel,
        grid=(b, N_HEADS // HEAD_GROUP, n_q),
        in_specs=[
            pl.BlockSpec((None, MOBA_BLOCK, width), lambda bi, hg, qi: (bi, qi, hg)),
            pl.BlockSpec((None, s, width), lambda bi, hg, qi: (bi, 0, hg),
                         pipeline_mode=pl.Buffered(1)),
            pl.BlockSpec((None, HEAD_GROUP * VT_ROWS, s), lambda bi, hg, qi: (bi, hg, 0),
                         pipeline_mode=pl.Buffered(1)),
        ],
        out_specs=pl.BlockSpec((None, MOBA_BLOCK, width), lambda bi, hg, qi: (bi, qi, hg)),
        out_shape=jax.ShapeDtypeStruct((b, s, d), BF16),
        scratch_shapes=[
            pltpu.VMEM((HEAD_GROUP, 2 * n_q, HEAD_DIM), BF16),
            pltpu.VMEM((HEAD_GROUP, 2 * HEAD_DIM, MOBA_BLOCK), BF16),
            pltpu.VMEM((HEAD_GROUP, 1, MOBA_BLOCK), F32),
            pltpu.VMEM((HEAD_GROUP, VT_ROWS, MOBA_BLOCK), F32),
            pltpu.VMEM((2, HEAD_GROUP, MOBA_BLOCK, MOBA_BLOCK), F32),
            pltpu.VMEM((2, HEAD_GROUP, 1, MOBA_BLOCK), F32),
            pltpu.VMEM((2, HEAD_GROUP, MOBA_BLOCK, MOBA_BLOCK), BF16),
            pltpu.VMEM((2, HEAD_GROUP, 1, MOBA_BLOCK), F32),
        ],
        compiler_params=pltpu.CompilerParams(
            dimension_semantics=("arbitrary", "arbitrary", "arbitrary"),
            vmem_limit_bytes=VMEM_LIMIT),
        name="moba_attn",
    )(q, k, vt)


def _rms(x, gain):
    return x * lax.rsqrt(jnp.mean(x * x, axis=-1, keepdims=True) + RMS_EPS) * gain


def _out_ffn_kernel(x_ref, a_ref, att_ref, sgc_ref, sga_ref, wc_ref, wa_ref, wo_ref,
                    g2_ref, wu_ref, wg_ref, fcw_ref, fcb_ref, wd_ref, g3_ref,
                    o_ref, tail_ref):
    si = pl.program_id(1)
    rows = x_ref.shape[0]

    y_conv = jnp.dot(a_ref[...], wc_ref[...], preferred_element_type=F32)
    y_attn = jnp.dot(att_ref[...], wa_ref[...], preferred_element_type=F32)
    merged = sgc_ref[...].astype(F32) * y_conv + sga_ref[...].astype(F32) * y_attn
    h = x_ref[...] + jnp.dot(merged.astype(BF16), wo_ref[...], preferred_element_type=F32)

    xn = _rms(h, g2_ref[...]).astype(BF16)
    up = jnp.dot(xn, wu_ref[...], preferred_element_type=F32)
    g = jnp.dot(xn, wg_ref[...], preferred_element_type=F32)
    prev_tail = jnp.where(si == 0, 0.0, tail_ref[...])
    back1, back2 = _shifted_rows(prev_tail, g)
    gc = (fcw_ref[2:3, :] * g + fcw_ref[1:2, :] * back1 + fcw_ref[0:1, :] * back2) + fcb_ref[...]
    tail_ref[...] = g[rows - CONV_HALO_ROWS:, :]
    act = (gc * (1.0 / (1.0 + jnp.exp(-gc))) * up).astype(BF16)
    h = h + jnp.dot(act, wd_ref[...], preferred_element_type=F32)

    o_ref[...] = _rms(h, g3_ref[...])


def _out_ffn(x, a, att, sgc, sga, w_conv_out, w_attn_out, w_out, gain_ffn, w_ffn_in,
             ffn_conv_w, ffn_conv_b, w_ffn_down, gain_final):
    b, s, d = x.shape
    tm = OUT_FFN_ROWS
    row_spec = pl.BlockSpec((None, tm, d), lambda bi, si: (bi, si, 0))
    const = lambda shape: pl.BlockSpec(shape, lambda bi, si: (0,) * len(shape),
                                       pipeline_mode=pl.Buffered(1))
    return pl.pallas_call(
        _out_ffn_kernel,
        grid=(b, s // tm),
        in_specs=[
            row_spec, row_spec, row_spec, row_spec, row_spec,
            const((d, d)), const((d, d)), const((d, d)),
            const((1, d)),
            const((d, D_FF)), const((d, D_FF)),
            const((3, D_FF)), const((1, D_FF)),
            const((D_FF, d)),
            const((1, d)),
        ],
        out_specs=row_spec,
        out_shape=jax.ShapeDtypeStruct((b, s, d), F32),
        scratch_shapes=[pltpu.VMEM((CONV_HALO_ROWS, D_FF), F32)],
        compiler_params=pltpu.CompilerParams(
            dimension_semantics=("arbitrary", "arbitrary"),
            vmem_limit_bytes=VMEM_LIMIT),
        name="out_ffn",
    )(x, a, att, sgc, sga,
      w_conv_out.astype(BF16), w_attn_out.astype(BF16), w_out.astype(BF16),
      gain_ffn[None, :],
      w_ffn_in[:, :D_FF].astype(BF16), w_ffn_in[:, D_FF:].astype(BF16),
      ffn_conv_w, ffn_conv_b[None, :], w_ffn_down.astype(BF16), gain_final[None, :])


@jax.jit
def kernel(x, positions, norm_mix_g, w_in, conv_w, w_conv_out, w_attn_out, w_out, norm_ffn_g,
           w_ffn_in, ffn_conv_w, ffn_conv_b, w_ffn_down, norm_final_g):
    assert norm_mix_g.shape[0] == 1, "single-layer stack"
    a, q, k, vt, sgc, sga = _in_proj(x, positions, norm_mix_g[0], w_in[0], conv_w[0])
    att = _moba_attention(q, k, vt)
    return _out_ffn(x, a, att, sgc, sga, w_conv_out[0], w_attn_out[0], w_out[0], norm_ffn_g[0],
                    w_ffn_in[0], ffn_conv_w[0], ffn_conv_b[0], w_ffn_down[0], norm_final_g)
```
